```python
import math
import jax, jax.numpy as jnp
from jax import lax
import numpy as np

D_MODEL = 1024
BATCH = 2
SEQ = 8192
DEPTH = 2
DEC_BATCH = 16
DEC_SEQ = 4096
PAST_LEN = 128

N_MIXERS = 2
N_EVEN = (DEPTH + 1) // 2
N_ODD = DEPTH // 2
SSM_GROUP = 16
SSM_GROUPS = D_MODEL // SSM_GROUP
SSM_STATE = 64
HEAD_DIM = 64
N_HEADS = D_MODEL // HEAD_DIM
DILATION_CONFIGS = ((128, 1), (512, 4), (2048, 16))
N_DIL_GROUPS = len(DILATION_CONFIGS)
ROPE_THETA = 10000.0
D_FF = (7 * D_MODEL) // 2
N_EXPERTS = 8
TOP_K = 2
MOE_BLOCK = 256
RMS_EPS = 1e-6
NEG_INF = -1e30

kernel_name = 'hybrid_s5_dilated_attn_encoder'


def rmsnorm(x, gain):
    xf = x.astype(jnp.float32)
    y = xf * lax.rsqrt(jnp.mean(xf * xf, axis=-1, keepdims=True) + RMS_EPS)
    return (y * gain.astype(jnp.float32)).astype(x.dtype)


def rope_tables(seq):
    inv_freq = ROPE_THETA ** (-jnp.arange(0, HEAD_DIM, 2, dtype=jnp.float32) / HEAD_DIM)
    ang = jnp.arange(seq, dtype=jnp.float32)[:, None] * inv_freq[None, :]
    return jnp.cos(ang), jnp.sin(ang)


def apply_rope(x, cos, sin):
    xf = x.astype(jnp.float32)
    x1, x2 = xf[..., : HEAD_DIM // 2], xf[..., HEAD_DIM // 2:]
    c = cos[None, :, None, None, :]
    s = sin[None, :, None, None, :]
    return jnp.concatenate([x1 * c - x2 * s, x2 * c + x1 * s], axis=-1).astype(x.dtype)


def _diag_combine(e1, e2):
    a1, b1 = e1
    a2, b2 = e2
    return a2 * a1, a2 * b1 + b2


def s5_bidirectional(u, lam_re, lam_im, log_dt, b_re, b_im, c_re, c_im, d_skip):
    seq = u.shape[1]
    lam = jnp.minimum(lam_re.astype(jnp.float32), -1e-4) + 1j * lam_im.astype(jnp.float32)
    dt = jnp.exp(log_dt.astype(jnp.float32))[..., None]
    lam_bar = jnp.exp(lam * dt)
    b_c = b_re.astype(jnp.float32) + 1j * b_im.astype(jnp.float32)
    b_bar = ((lam_bar - 1.0) / lam)[..., None] * b_c
    c_c = c_re.astype(jnp.float32) + 1j * c_im.astype(jnp.float32)

    def one_sequence(us):
        ug = us.astype(jnp.float32).reshape(seq, SSM_GROUPS, SSM_GROUP).astype(jnp.complex64)
        ys = []
        for direction in range(2):
            bu = jnp.einsum('sgh,gph->sgp', ug, b_bar[direction])
            a = jnp.broadcast_to(lam_bar[direction], bu.shape)
            _, states = lax.associative_scan(_diag_combine, (a, bu),
                                             reverse=(direction == 1), axis=0)
            ys.append(jnp.einsum('sgp,ghp->sgh', states, c_c[direction]).real)
        return (ys[0] + ys[1]).reshape(seq, D_MODEL)

    y = lax.map(one_sequence, u)
    return (y + d_skip.astype(jnp.float32) * u.astype(jnp.float32)).astype(u.dtype)


def s5_mixer(h, lam_re, lam_im, log_dt, b_re, b_im, c_re, c_im, d_skip, w_glu):
    y = s5_bidirectional(h, lam_re, lam_im, log_dt, b_re, b_im, c_re, c_im, d_skip)
    z = jax.nn.gelu(y) @ w_glu
    return z[..., :D_MODEL] * jax.nn.sigmoid(z[..., D_MODEL:])


def dilated_group_attention(q, k, v, dil, half):
    bsz, seq, nh, hd = q.shape
    blk = half
    n_sub = -(-seq // (dil * blk)) * blk
    s_pad = n_sub * dil
    nb = n_sub // blk
    pad = ((0, 0), (0, s_pad - seq), (0, 0), (0, 0))

    def to_sub(t):
        return jnp.pad(t, pad).reshape(bsz, n_sub, dil, nh, hd)

    qs, ks, vs = to_sub(q), to_sub(k), to_sub(v)
    qb = qs.reshape(bsz, nb, blk, dil, nh, hd)

    def windows(t):
        te = jnp.pad(t, ((0, 0), (blk, blk), (0, 0), (0, 0), (0, 0)))
        return jnp.concatenate(
            [te[:, j * blk: j * blk + n_sub].reshape(bsz, nb, blk, dil, nh, hd) for j in range(3)],
            axis=2)

    kw, vw = windows(ks), windows(vs)
    qa = jnp.arange(blk)[:, None]
    kc = jnp.arange(3 * blk)[None, :]
    band = jnp.abs(kc - blk - qa) <= half
    kl = jnp.arange(nb)[:, None] * blk - blk + jnp.arange(3 * blk)[None, :]
    kpos = kl[:, :, None] * dil + jnp.arange(dil)[None, None, :]
    kvalid = (kl[:, :, None] >= 0) & (kpos < seq)
    mask = band[None, None, :, :] & jnp.transpose(kvalid, (0, 2, 1))[:, :, None, :]

    scores = jnp.einsum('bnqrhd,bnkrhd->bnrhqk', qb, kw,
                        preferred_element_type=jnp.float32) * (hd ** -0.5)
    scores = jnp.where(mask[None, :, :, None, :, :], scores, NEG_INF)
    m = jnp.max(scores, axis=-1, keepdims=True)
    p = jnp.exp(scores - m)
    den = jnp.sum(p, axis=-1)
    out = jnp.einsum('bnrhqk,bnkrhd->bnqrhd', p, vw.astype(jnp.float32))
    out = out / jnp.transpose(den, (0, 1, 4, 2, 3))[..., None]
    lse = jnp.transpose(m[..., 0] + jnp.log(den), (0, 1, 4, 2, 3))
    out = out.reshape(bsz, s_pad, nh, hd)[:, :seq]
    lse = lse.reshape(bsz, s_pad, nh)[:, :seq]
    return out, lse


def dilated_attention_mixer(h, w_qkv, q_gain, k_gain, w_o):
    bsz, seq, _ = h.shape
    qkv = (h @ w_qkv).reshape(bsz, seq, 3, N_DIL_GROUPS, N_HEADS, HEAD_DIM)
    cos, sin = rope_tables(seq)
    q = apply_rope(rmsnorm(qkv[:, :, 0], q_gain), cos, sin)
    k = apply_rope(rmsnorm(qkv[:, :, 1], k_gain), cos, sin)
    v = qkv[:, :, 2]
    outs, lses = [], []
    for g, (window, dil) in enumerate(DILATION_CONFIGS):
        o, l = dilated_group_attention(q[:, :, g], k[:, :, g], v[:, :, g], dil, window // 2 // dil)
        outs.append(o)
        lses.append(l)
    wts = jax.nn.softmax(jnp.stack(lses), axis=0)
    merged = jnp.einsum('gbsh,gbshd->bshd', wts, jnp.stack(outs))
    return merged.astype(h.dtype).reshape(bsz, seq, N_HEADS * HEAD_DIM) @ w_o


def swiglu(h, w_gu, w_down):
    g, u = jnp.split(h @ w_gu, 2, axis=-1)
    return (jax.nn.silu(g) * u) @ w_down


def moe_swiglu(h, w_router, w_gu, w_down):
    T = h.shape[0]
    logits = (h @ w_router).astype(jnp.float32)
    top_val, top_idx = lax.top_k(logits, TOP_K)
    gates = jax.nn.softmax(top_val, axis=-1)
    n_assign = T * TOP_K
    e_flat = top_idx.reshape(n_assign).astype(jnp.int32)
    tok_flat = jnp.repeat(jnp.arange(T, dtype=jnp.int32), TOP_K)
    g_flat = gates.reshape(n_assign)
    order = jnp.argsort(e_flat)
    e_sorted = e_flat[order]
    counts = jnp.bincount(e_flat, length=N_EXPERTS).astype(jnp.int32)
    padded = ((counts + MOE_BLOCK - 1) // MOE_BLOCK) * MOE_BLOCK
    start = jnp.cumsum(counts) - counts
    pend = jnp.cumsum(padded)
    pstart = pend - padded
    dest = pstart[e_sorted] + (jnp.arange(n_assign, dtype=jnp.int32) - start[e_sorted])
    n_blocks = -(-n_assign // MOE_BLOCK) + N_EXPERTS
    n_rows = n_blocks * MOE_BLOCK
    row_tok = jnp.full((n_rows,), T, dtype=jnp.int32).at[dest].set(tok_flat[order])
    row_gate = jnp.zeros((n_rows,), jnp.float32).at[dest].set(g_flat[order])
    block_start = jnp.arange(n_blocks, dtype=jnp.int32) * MOE_BLOCK
    block_exp = jnp.clip(jnp.searchsorted(pend, block_start, side='right'), 0, N_EXPERTS - 1)
    h_ext = jnp.concatenate([h, jnp.zeros((1, h.shape[1]), h.dtype)], axis=0)
    xs = h_ext[row_tok].reshape(n_blocks, MOE_BLOCK, h.shape[1])

    def expert_block(args):
        xb, e = args
        return swiglu(xb, w_gu[e], w_down[e])

    ys = lax.map(expert_block, (xs, block_exp)).reshape(n_rows, h.shape[1])
    ys = ys * row_gate[:, None].astype(ys.dtype)
    return jax.ops.segment_sum(ys, row_tok, num_segments=T + 1)[:T].astype(h.dtype)


def trunk(x, ssm_norm, ssm_lam_re, ssm_lam_im, ssm_log_dt, ssm_b_re, ssm_b_im, ssm_c_re,
          ssm_c_im, ssm_d, ssm_w_glu, ffn_norm, ffn_w_gu, ffn_w_down, att_norm, att_w_qkv,
          att_q_gain, att_k_gain, att_w_o, moe_norm, moe_w_router, moe_w_gu, moe_w_down):
    bsz, seq, _ = x.shape
    for i in range(DEPTH):
        j = i // 2
        if i % N_MIXERS == 0:
            h = rmsnorm(x, ssm_norm[j])
            x = x + s5_mixer(h, ssm_lam_re[j], ssm_lam_im[j], ssm_log_dt[j], ssm_b_re[j],
                             ssm_b_im[j], ssm_c_re[j], ssm_c_im[j], ssm_d[j], ssm_w_glu[j])
        else:
            h = rmsnorm(x, att_norm[j])
            x = x + dilated_attention_mixer(h, att_w_qkv[j], att_q_gain[j], att_k_gain[j], att_w_o[j])
        if i % 2 == 0:
            x = x + swiglu(rmsnorm(x, ffn_norm[j]), ffn_w_gu[j], ffn_w_down[j])
        else:
            h = rmsnorm(x, moe_norm[j]).reshape(bsz * seq, D_MODEL)
            x = x + moe_swiglu(h, moe_w_router[j], moe_w_gu[j], moe_w_down[j]).reshape(bsz, seq, D_MODEL)
    return x


def setup_inputs(seed: int = 0) -> dict:
    key = jax.random.key(seed)
    ks = jax.random.split(key, 24)
    f32 = jnp.float32
    nrm = lambda k, shape, s: jax.random.normal(k, shape, f32) * s
    gain = lambda k, shape: 1.0 + 0.02 * jax.random.normal(k, shape, f32)
    lam_im_base = jnp.pi * jnp.arange(SSM_STATE, dtype=f32)
    return {
        'x_prompt': jax.random.normal(ks[0], (BATCH, SEQ, D_MODEL), f32),
        'x_sample': jax.random.normal(ks[1], (DEC_BATCH, DEC_SEQ, D_MODEL), f32),
        'ssm_norm': gain(ks[2], (N_EVEN, D_MODEL)),
        'ssm_lam_re': -0.5 + nrm(ks[3], (N_EVEN, 2, SSM_GROUPS, SSM_STATE), 0.01),
        'ssm_lam_im': lam_im_base + nrm(ks[4], (N_EVEN, 2, SSM_GROUPS, SSM_STATE), 0.01),
        'ssm_log_dt': jax.random.uniform(ks[5], (N_EVEN, 2, SSM_GROUPS), f32,
                                         math.log(1e-3), math.log(1e-1)),
        'ssm_b_re': nrm(ks[6], (N_EVEN, 2, SSM_GROUPS, SSM_STATE, SSM_GROUP), (2 * SSM_GROUP) ** -0.5),
        'ssm_b_im': nrm(ks[7], (N_EVEN, 2, SSM_GROUPS, SSM_STATE, SSM_GROUP), (2 * SSM_GROUP) ** -0.5),
        'ssm_c_re': nrm(ks[8], (N_EVEN, 2, SSM_GROUPS, SSM_GROUP, SSM_STATE), (2 * SSM_STATE) ** -0.5),
        'ssm_c_im': nrm(ks[9], (N_EVEN, 2, SSM_GROUPS, SSM_GROUP, SSM_STATE), (2 * SSM_STATE) ** -0.5),
        'ssm_d': nrm(ks[10], (N_EVEN, D_MODEL), 1.0),
        'ssm_w_glu': nrm(ks[11], (N_EVEN, D_MODEL, 2 * D_MODEL), D_MODEL ** -0.5),
        'ffn_norm': gain(ks[12], (N_EVEN, D_MODEL)),
        'ffn_w_gu': nrm(ks[13], (N_EVEN, D_MODEL, 2 * D_FF), D_MODEL ** -0.5),
        'ffn_w_down': nrm(ks[14], (N_EVEN, D_FF, D_MODEL), D_FF ** -0.5),
        'att_norm': gain(ks[15], (N_ODD, D_MODEL)),
        'att_w_qkv': nrm(ks[16], (N_ODD, D_MODEL, 3 * N_DIL_GROUPS * N_HEADS * HEAD_DIM), D_MODEL ** -0.5),
        'att_q_gain': gain(ks[17], (N_ODD, HEAD_DIM)),
        'att_k_gain': gain(ks[18], (N_ODD, HEAD_DIM)),
        'att_w_o': nrm(ks[19], (N_ODD, N_HEADS * HEAD_DIM, D_MODEL), (N_HEADS * HEAD_DIM) ** -0.5),
        'moe_norm': gain(ks[20], (N_ODD, D_MODEL)),
        'moe_w_router': nrm(ks[21], (N_ODD, D_MODEL, N_EXPERTS), D_MODEL ** -0.5),
        'moe_w_gu': nrm(ks[22], (N_ODD, N_EXPERTS, D_MODEL, 2 * D_FF), D_MODEL ** -0.5),
        'moe_w_down': nrm(ks[23], (N_ODD, N_EXPERTS, D_FF, D_MODEL), D_FF ** -0.5),
    }


def reference(x_prompt, x_sample, ssm_norm, ssm_lam_re, ssm_lam_im, ssm_log_dt, ssm_b_re,
              ssm_b_im, ssm_c_re, ssm_c_im, ssm_d, ssm_w_glu, ffn_norm, ffn_w_gu, ffn_w_down,
              att_norm, att_w_qkv, att_q_gain, att_k_gain, att_w_o, moe_norm, moe_w_router,
              moe_w_gu, moe_w_down):
    weights = (ssm_norm, ssm_lam_re, ssm_lam_im, ssm_log_dt, ssm_b_re, ssm_b_im, ssm_c_re,
               ssm_c_im, ssm_d, ssm_w_glu, ffn_norm, ffn_w_gu, ffn_w_down, att_norm, att_w_qkv,
               att_q_gain, att_k_gain, att_w_o, moe_norm, moe_w_router, moe_w_gu, moe_w_down)
    y_prompt = trunk(x_prompt, *weights)
    y_sample = trunk(x_sample, *weights)
    return (y_prompt, y_sample)
```

```python
import functools
import math

import jax
import jax.numpy as jnp
from jax import lax
from jax.experimental import pallas as pl
from jax.experimental.pallas import tpu as pltpu

F32 = jnp.float32
BF16 = jnp.bfloat16

D_MODEL = 1024
SSM_GROUP = 16
SSM_GROUPS = 64
SSM_STATE = 64
HEAD_DIM = 64
N_HEADS = 16
DILATIONS = (1, 4, 16)
BAND_HALF = 64
ROPE_THETA = 10000.0
D_FF = 3584
N_EXPERTS = 8
RMS_EPS = 1e-6
NEG_INF = -1e30

CHUNK = 16
LANES = 128
GROUPS_PER_TILE = LANES // SSM_GROUP
N_GTILES = D_MODEL // LANES
S5_ROWS = 512
FF_TILE = 512
ROW_TILE = 1024
MOE_ROWS = 512
ATT_Q = 128
VMEM_LIMIT = 56 * 1024 * 1024


def _params(sem, vmem=VMEM_LIMIT):
    return pltpu.CompilerParams(dimension_semantics=sem, vmem_limit_bytes=vmem)


def _rms(x, gain):
    ms = jnp.mean(x * x, axis=-1, keepdims=True)
    return x * lax.rsqrt(ms + RMS_EPS) * gain


def _rmsnorm_kernel(x_ref, g_ref, o_ref):
    o_ref[...] = _rms(x_ref[...], g_ref[...]).astype(o_ref.dtype)


def rmsnorm_rows(x, gain):
    t = x.shape[0]
    return pl.pallas_call(
        _rmsnorm_kernel,
        grid=(t // ROW_TILE,),
        in_specs=[pl.BlockSpec((ROW_TILE, D_MODEL), lambda i: (i, 0)),
                  pl.BlockSpec((1, D_MODEL), lambda i: (0, 0))],
        out_specs=pl.BlockSpec((ROW_TILE, D_MODEL), lambda i: (i, 0)),
        out_shape=jax.ShapeDtypeStruct((t, D_MODEL), BF16),
        compiler_params=_params(("parallel",)),
        name="rmsnorm_rows",
    )(x, gain.reshape(1, D_MODEL))


def _s5_tables(lam_re, lam_im, log_dt, b_re, b_im, c_re, c_im):
    lam = jnp.minimum(lam_re.astype(F32), -1e-4) + 1j * lam_im.astype(F32)
    dt = jnp.exp(log_dt.astype(F32))[..., None]
    z = lam * dt
    lam_bar = jnp.exp(z)
    b_bar = ((lam_bar - 1.0) / lam)[..., None] * (b_re.astype(F32) + 1j * b_im.astype(F32))
    c_c = c_re.astype(F32) + 1j * c_im.astype(F32)
    steps = jnp.arange(CHUNK, dtype=F32)[:, None, None]

    def power(direction, exponent):
        return jnp.exp(z[direction][None] * exponent)

    pw = jnp.stack([power(0, steps), power(1, steps)], 1)
    eye = jnp.eye(GROUPS_PER_TILE, dtype=F32)
    a8 = (N_GTILES, GROUPS_PER_TILE)

    kd = jnp.einsum('dgop,kdgp,dgph->dkgoh', c_c, pw, b_bar).real
    ii = jnp.arange(CHUNK)[:, None]
    jj = jnp.arange(CHUNK)[None, :]
    kf = kd[0][jnp.clip(ii - jj, 0, CHUNK - 1)]
    kb = kd[1][jnp.clip(jj - ii, 0, CHUNK - 1)]
    sel = lambda m: m[:, :, None, None, None]
    kmat = jnp.where(sel(ii >= jj), kf, 0.0) + jnp.where(sel(jj >= ii), kb, 0.0)
    kmat = kmat.reshape(CHUNK, CHUNK, *a8, SSM_GROUP, SSM_GROUP)
    w_toep = jnp.einsum('ijAgoh,gq->Ajghiqo', kmat, eye).reshape(N_GTILES, 2048, 2048)

    cf = power(0, CHUNK - 1.0 - steps)
    cb = power(1, steps)
    coef = jnp.stack([cf[..., None] * b_bar[0][None], cb[..., None] * b_bar[1][None]], 0)
    coef = jnp.stack([coef.real, coef.imag], 0).reshape(2, 2, CHUNK, *a8, SSM_STATE, SSM_GROUP)
    w_in = jnp.einsum('rdjAgph,gq->Adjghrqp', coef, eye).reshape(N_GTILES, 2, 2048, 1024)

    pf = power(0, steps + 1.0)
    pb = power(1, CHUNK - steps)
    w = jnp.stack([c_c[0][None] * pf[:, :, None, :], c_c[1][None] * pb[:, :, None, :]], 0)
    w = jnp.stack([w.real, -w.imag], 0).reshape(2, 2, CHUNK, *a8, SSM_GROUP, SSM_STATE)
    w_out = jnp.einsum('rdiAgop,gq->Adrgpiqo', w, eye).reshape(N_GTILES, 2048, 2048)

    lc = jnp.exp(z * float(CHUNK))
    lam_c = jnp.stack([lc.real, lc.imag], 0).reshape(2, 2, *a8, SSM_STATE)
    lam_c = jnp.transpose(lam_c, (2, 1, 0, 3, 4)).reshape(N_GTILES, 2, 2, 512)
    return w_in.astype(BF16), lam_c, w_toep.astype(BF16), w_out.astype(BF16)


def _chunk_scan(xr, xi, ar, ai, seq_rows, reverse):
    n = xr.shape[0]
    row = lax.broadcasted_iota(jnp.int32, xr.shape, 0) % seq_rows

    def shift(x, sh):
        if reverse:
            y = pltpu.roll(x, n - sh, 0)
            return jnp.where(row < seq_rows - sh, y, 0.0)
        y = pltpu.roll(x, sh, 0)
        return jnp.where(row >= sh, y, 0.0)

    xr, xi = shift(xr, 1), shift(xi, 1)
    sh = 1
    while sh < seq_rows:
        yr, yi = shift(xr, sh), shift(xi, sh)
        xr, xi = xr + ar * yr - ai * yi, xi + ar * yi + ai * yr
        ar, ai = ar * ar - ai * ai, 2.0 * ar * ai
        sh *= 2
    return xr, xi


def _s5a_kernel(h_ref, win_ref, lam_ref, c_ref, *, seq_rows):
    u = jnp.concatenate([h_ref[t] for t in range(CHUNK)], axis=-1)
    s = jnp.dot(u, win_ref[...], preferred_element_type=F32)
    half = s.shape[1] // 2
    ar = lam_ref[0:1, :]
    ai = lam_ref[1:2, :]
    for direction in range(2):
        @pl.when(pl.program_id(1) == direction)
        def _(direction=direction):
            xr, xi = _chunk_scan(s[:, :half], s[:, half:], ar, ai, seq_rows, direction == 1)
            c_ref[:, :half] = xr.astype(c_ref.dtype)
            c_ref[:, half:] = xi.astype(c_ref.dtype)


def s5_states(ht, w_in, lam_c, seq_rows):
    nc = ht.shape[1]
    n = S5_ROWS
    return pl.pallas_call(
        functools.partial(_s5a_kernel, seq_rows=seq_rows),
        grid=(N_GTILES, 2, nc // n),
        in_specs=[pl.BlockSpec((CHUNK, n, LANES), lambda a, d, r: (0, r, a)),
                  pl.BlockSpec((None, None, 2048, 1024), lambda a, d, r: (a, d, 0, 0)),
                  pl.BlockSpec((None, None, 2, 512), lambda a, d, r: (a, d, 0, 0))],
        out_specs=pl.BlockSpec((None, None, n, 1024), lambda a, d, r: (a, d, r, 0)),
        out_shape=jax.ShapeDtypeStruct((N_GTILES, 2, nc, 1024), BF16),
        compiler_params=_params(("parallel", "parallel", "parallel")),
        name="s5_states",
    )(ht, w_in, lam_c)


def _gelu_tanh(x):
    return 0.5 * x * (1.0 + jnp.tanh(math.sqrt(2.0 / math.pi) * (x + 0.044715 * (x * x * x))))


def _s5b_kernel(h_ref, c_ref, wt_ref, wo_ref, d_ref, o_ref):
    half_steps = o_ref.shape[0]
    u = jnp.concatenate([h_ref[t] for t in range(CHUNK)], axis=-1)
    cc = jnp.concatenate([c_ref[0], c_ref[1]], axis=-1)
    y = (jnp.dot(u, wt_ref[...], preferred_element_type=F32)
         + jnp.dot(cc, wo_ref[...], preferred_element_type=F32))
    base = pl.program_id(1) * half_steps
    for i in range(half_steps):
        ht = h_ref[base + i].astype(F32)
        yi = y[:, i * LANES:(i + 1) * LANES] + d_ref[...] * ht
        o_ref[i] = _gelu_tanh(yi).astype(o_ref.dtype)


def s5_outputs(ht, cg, w_toep, w_out, d_skip):
    nc = ht.shape[1]
    n = S5_ROWS
    hs = CHUNK // 2
    return pl.pallas_call(
        _s5b_kernel,
        grid=(N_GTILES, 2, nc // n),
        in_specs=[pl.BlockSpec((CHUNK, n, LANES), lambda a, h, r: (0, r, a)),
                  pl.BlockSpec((None, 2, n, 1024), lambda a, h, r: (a, 0, r, 0)),
                  pl.BlockSpec((None, 2048, 1024), lambda a, h, r: (a, 0, h)),
                  pl.BlockSpec((None, 2048, 1024), lambda a, h, r: (a, 0, h)),
                  pl.BlockSpec((None, 1, LANES), lambda a, h, r: (a, 0, 0))],
        out_specs=pl.BlockSpec((hs, n, LANES), lambda a, h, r: (h, r, a)),
        out_shape=jax.ShapeDtypeStruct((CHUNK, nc, D_MODEL), BF16),
        compiler_params=_params(("parallel", "parallel", "parallel")),
        name="s5_outputs",
    )(ht, cg, w_toep, w_out, d_skip.astype(F32).reshape(N_GTILES, 1, LANES))


def _sigmoid(x):
    return 1.0 / (1.0 + jnp.exp(-x))


def _glu_kernel(g_ref, x_ref, w_ref, o_ref):
    z = jnp.dot(g_ref[...], w_ref[...], preferred_element_type=F32)
    o_ref[...] = x_ref[...] + z[:, :D_MODEL] * _sigmoid(z[:, D_MODEL:])


def glu_residual(g, x, w_glu):
    t = x.shape[0]
    tm = ROW_TILE // 2
    return pl.pallas_call(
        _glu_kernel,
        grid=(t // tm,),
        in_specs=[pl.BlockSpec((tm, D_MODEL), lambda i: (i, 0)),
                  pl.BlockSpec((tm, D_MODEL), lambda i: (i, 0)),
                  pl.BlockSpec((D_MODEL, 2 * D_MODEL), lambda i: (0, 0))],
        out_specs=pl.BlockSpec((tm, D_MODEL), lambda i: (i, 0)),
        out_shape=jax.ShapeDtypeStruct((t, D_MODEL), F32),
        compiler_params=_params(("parallel",)),
        name="glu_residual",
    )(g, x, w_glu)


def _swiglu_step(h, wg_ref, wu_ref, wd_ref, acc_ref):
    g = jnp.dot(h, wg_ref[...], preferred_element_type=F32)
    u = jnp.dot(h, wu_ref[...], preferred_element_type=F32)
    a = (g * _sigmoid(g) * u).astype(BF16)
    acc_ref[...] += jnp.dot(a, wd_ref[...], preferred_element_type=F32)


def _ffn_kernel(x_ref, n_ref, wg_ref, wu_ref, wd_ref, o_ref, h_scr, acc_scr):
    f = pl.program_id(1)

    @pl.when(f == 0)
    def _():
        h_scr[...] = _rms(x_ref[...], n_ref[...]).astype(BF16)
        acc_scr[...] = jnp.zeros_like(acc_scr)

    _swiglu_step(h_scr[...], wg_ref, wu_ref, wd_ref, acc_scr)

    @pl.when(f == pl.num_programs(1) - 1)
    def _():
        o_ref[...] = x_ref[...] + acc_scr[...]


def ffn_residual(x, gain, w_gu, w_down):
    t = x.shape[0]
    nf = D_FF // FF_TILE
    return pl.pallas_call(
        _ffn_kernel,
        grid=(t // ROW_TILE, nf),
        in_specs=[pl.BlockSpec((ROW_TILE, D_MODEL), lambda i, f: (i, 0)),
                  pl.BlockSpec((1, D_MODEL), lambda i, f: (0, 0)),
                  pl.BlockSpec((D_MODEL, FF_TILE), lambda i, f: (0, f)),
                  pl.BlockSpec((D_MODEL, FF_TILE), lambda i, f: (0, nf + f)),
                  pl.BlockSpec((FF_TILE, D_MODEL), lambda i, f: (f, 0))],
        out_specs=pl.BlockSpec((ROW_TILE, D_MODEL), lambda i, f: (i, 0)),
        out_shape=jax.ShapeDtypeStruct((t, D_MODEL), F32),
        scratch_shapes=[pltpu.VMEM((ROW_TILE, D_MODEL), BF16),
                        pltpu.VMEM((ROW_TILE, D_MODEL), F32)],
        compiler_params=_params(("parallel", "arbitrary")),
        name="ffn_residual",
    )(x, gain.reshape(1, D_MODEL), w_gu, w_gu, w_down)


def _expert_kernel(be_ref, x_ref, wg_ref, wu_ref, wd_ref, o_ref, acc_scr):
    f = pl.program_id(1)

    @pl.when(f == 0)
    def _():
        acc_scr[...] = jnp.zeros_like(acc_scr)

    _swiglu_step(x_ref[...], wg_ref, wu_ref, wd_ref, acc_scr)

    @pl.when(f == pl.num_programs(1) - 1)
    def _():
        o_ref[...] = acc_scr[...]


def expert_swiglu(xs, block_exp, w_gu, w_down):
    rows = xs.shape[0]
    nf = D_FF // FF_TILE
    grid_spec = pltpu.PrefetchScalarGridSpec(
        num_scalar_prefetch=1,
        grid=(rows // MOE_ROWS, nf),
        in_specs=[pl.BlockSpec((MOE_ROWS, D_MODEL), lambda i, f, be: (i, 0)),
                  pl.BlockSpec((None, D_MODEL, FF_TILE), lambda i, f, be: (be[i], 0, f)),
                  pl.BlockSpec((None, D_MODEL, FF_TILE), lambda i, f, be: (be[i], 0, nf + f)),
                  pl.BlockSpec((None, FF_TILE, D_MODEL), lambda i, f, be: (be[i], f, 0))],
        out_specs=pl.BlockSpec((MOE_ROWS, D_MODEL), lambda i, f, be: (i, 0)),
        scratch_shapes=[pltpu.VMEM((MOE_ROWS, D_MODEL), F32)],
    )
    return pl.pallas_call(
        _expert_kernel,
        grid_spec=grid_spec,
        out_shape=jax.ShapeDtypeStruct((rows, D_MODEL), F32),
        compiler_params=_params(("parallel", "arbitrary")),
        name="expert_swiglu",
    )(block_exp, xs, w_gu, w_gu, w_down)


def _quad_perm():
    idx = []
    for c in range(N_HEADS // 4):
        for half in range(2):
            for j in range(4):
                for f in range(HEAD_DIM // 2):
                    idx.append((4 * c + j) * HEAD_DIM + half * (HEAD_DIM // 2) + f)
    return jnp.asarray(idx, dtype=jnp.int32)


def _qkv_kernel(x_ref, n_ref, w_ref, cos_ref, sin_ref, hg_ref, ones_ref, o_ref, h_scr):
    j = pl.program_id(1)

    @pl.when(j == 0)
    def _():
        h_scr[...] = _rms(x_ref[...], n_ref[...]).astype(BF16)

    acc = jnp.dot(h_scr[...], w_ref[...], preferred_element_type=F32)

    @pl.when(j >= 6)
    def _():
        o_ref[...] = acc.astype(o_ref.dtype)

    @pl.when(j < 6)
    def _():
        gains = hg_ref[jnp.where(j >= 3, 1, 0)]
        ga, gb = gains[0:1, :], gains[1:2, :]
        cos, sin = cos_ref[...], sin_ref[...]
        for c in range(N_HEADS // 4):
            lo = c * 2 * LANES
            a = acc[:, lo:lo + LANES]
            b = acc[:, lo + LANES:lo + 2 * LANES]
            sq = jnp.concatenate([a * a, b * b], axis=-1).astype(BF16)
            ss = jnp.dot(sq, ones_ref[...], preferred_element_type=F32)
            inv = lax.rsqrt(ss * (1.0 / HEAD_DIM) + RMS_EPS)
            an = a * inv * ga
            bn = b * inv * gb
            o_ref[:, lo:lo + LANES] = (an * cos - bn * sin).astype(o_ref.dtype)
            o_ref[:, lo + LANES:lo + 2 * LANES] = (bn * cos + an * sin).astype(o_ref.dtype)


def qkv_project(x, gain, w_qkv_p, cos_t, sin_t, head_gains, seq):
    t = x.shape[0]
    tm = ROW_TILE
    lane = jnp.arange(2 * LANES)
    ones = ((lane[:, None] % LANES) // 32 == (jnp.arange(LANES)[None, :] // 32)).astype(BF16)
    pos_blocks = seq // tm
    return pl.pallas_call(
        _qkv_kernel,
        grid=(t // tm, 9),
        in_specs=[pl.BlockSpec((tm, D_MODEL), lambda i, j: (i, 0)),
                  pl.BlockSpec((1, D_MODEL), lambda i, j: (0, 0)),
                  pl.BlockSpec((D_MODEL, D_MODEL), lambda i, j: (0, j)),
                  pl.BlockSpec((tm, LANES), lambda i, j: (i % pos_blocks, 0)),
                  pl.BlockSpec((tm, LANES), lambda i, j: (i % pos_blocks, 0)),
                  pl.BlockSpec((2, 2, LANES), lambda i, j: (0, 0, 0)),
                  pl.BlockSpec((2 * LANES, LANES), lambda i, j: (0, 0))],
        out_specs=pl.BlockSpec((tm, D_MODEL), lambda i, j: (i, j)),
        out_shape=jax.ShapeDtypeStruct((t, 9 * D_MODEL), BF16),
        scratch_shapes=[pltpu.VMEM((tm, D_MODEL), BF16)],
        compiler_params=_params(("parallel", "arbitrary")),
        name="qkv_project",
    )(x, gain.reshape(1, D_MODEL), w_qkv_p, cos_t, sin_t, head_gains, ones)


def _attn_kernel(q_ref, kp_ref, kc_ref, kn_ref, vp_ref, vc_ref, vn_ref, bias_ref, o_ref, lse_ref):
    quad = 2 * LANES
    lane_q = lax.broadcasted_iota(jnp.int32, (1, quad), 1)
    q_head = (lane_q % LANES) // 32
    v_head = lane_q // HEAD_DIM
    lane_l = lax.broadcasted_iota(jnp.int32, (1, LANES), 1)
    bias = bias_ref[...]
    lse_acc = jnp.zeros(lse_ref.shape, F32)
    for c in range(N_HEADS // 4):
        sl = slice(c * quad, (c + 1) * quad)
        q = q_ref[:, sl]
        k = jnp.concatenate([kp_ref[:, sl], kc_ref[:, sl], kn_ref[:, sl]], axis=0)
        v = jnp.concatenate([vp_ref[:, sl], vc_ref[:, sl], vn_ref[:, sl]], axis=0)
        probs, vparts = [], []
        scale = jnp.zeros((q.shape[0], quad), F32)
        for j in range(4):
            qm = jnp.where(q_head == j, q, jnp.zeros_like(q))
            s = lax.dot_general(qm, k, (((1,), (1,)), ((), ())), preferred_element_type=F32) + bias
            m = jnp.max(s, axis=-1, keepdims=True)
            p = jnp.exp(s - m)
            l = jnp.sum(p, axis=-1, keepdims=True)
            probs.append(p.astype(BF16))
            vparts.append(jnp.where(v_head == j, v, jnp.zeros_like(v)))
            scale = jnp.where(v_head == j, 1.0 / l, scale)
            lse_acc = jnp.where(lane_l == 4 * c + j, m + jnp.log(l), lse_acc)
        pv = jnp.dot(jnp.concatenate(probs, axis=1), jnp.concatenate(vparts, axis=0),
                     preferred_element_type=F32)
        o_ref[:, sl] = (pv * scale).astype(o_ref.dtype)
    lse_ref[...] = lse_acc


def _band_bias():
    qa = jnp.arange(ATT_Q)[:, None]
    kc = jnp.arange(ATT_Q + 2 * BAND_HALF)[None, :]
    band = jnp.abs(kc - BAND_HALF - qa) <= BAND_HALF
    first = band & (kc >= BAND_HALF)
    last = band & (kc < BAND_HALF + ATT_Q)
    bias = jnp.stack([first, band, last], 0)
    return jnp.where(bias, 0.0, NEG_INF).astype(F32)


def dilated_attention(qkv, group, bsz, seq, bias):
    dil = DILATIONS[group]
    n_sub = seq // dil
    n_i = n_sub // ATT_Q
    half_blocks = ATT_Q // BAND_HALF
    qkv3 = qkv.reshape(bsz, n_sub, dil * 9 * D_MODEL)

    def col(which):
        return lambda b, r, i: (b, i, r * 9 + 3 * which + group)

    def col_prev(which):
        return lambda b, r, i: (b, jnp.maximum(i * half_blocks - 1, 0), r * 9 + 3 * which + group)

    def col_next(which):
        return lambda b, r, i: (b, jnp.minimum((i + 1) * half_blocks, n_sub // BAND_HALF - 1),
                                r * 9 + 3 * which + group)

    cur = lambda which: pl.BlockSpec((None, ATT_Q, D_MODEL), col(which))
    prev = lambda which: pl.BlockSpec((None, BAND_HALF, D_MODEL), col_prev(which))
    nxt = lambda which: pl.BlockSpec((None, BAND_HALF, D_MODEL), col_next(which))
    bias_spec = pl.BlockSpec(
        (None, ATT_Q, ATT_Q + 2 * BAND_HALF),
        lambda b, r, i: (jnp.where(i == 0, 0, jnp.where(i == n_i - 1, 2, 1)), 0, 0))
    out, lse = pl.pallas_call(
        _attn_kernel,
        grid=(bsz, dil, n_i),
        in_specs=[cur(0), prev(1), cur(1), nxt(1), prev(2), cur(2), nxt(2), bias_spec],
        out_specs=[pl.BlockSpec((None, ATT_Q, D_MODEL), lambda b, r, i: (b, i, r)),
                   pl.BlockSpec((None, ATT_Q, LANES), lambda b, r, i: (b, i, r))],
        out_shape=[jax.ShapeDtypeStruct((bsz, n_sub, dil * D_MODEL), BF16),
                   jax.ShapeDtypeStruct((bsz, n_sub, dil * LANES), F32)],
        compiler_params=_params(("parallel", "parallel", "parallel")),
        name=f"dilated_attention_{dil}",
    )(qkv3, qkv3, qkv3, qkv3, qkv3, qkv3, qkv3, bias)
    return out.reshape(bsz * seq, D_MODEL), lse.reshape(bsz * seq, LANES)


def _split_bf16(x):
    hi = x.astype(BF16)
    return hi, (x - hi.astype(F32)).astype(BF16)


def _merge_kernel(o0_ref, o1_ref, o2_ref, l0_ref, l1_ref, l2_ref, x_ref, wo_ref, e_ref, n_ref,
                  rh_ref, rl_ref, x3_ref, hm_ref, rt_ref):
    la, lb, lc = l0_ref[...], l1_ref[...], l2_ref[...]
    m = jnp.maximum(jnp.maximum(la, lb), lc)
    ea, eb, ec = jnp.exp(la - m), jnp.exp(lb - m), jnp.exp(lc - m)
    inv = 1.0 / (ea + eb + ec)
    merged = jnp.zeros(x_ref.shape, F32)
    for e, o_ref in ((ea, o0_ref), (eb, o1_ref), (ec, o2_ref)):
        hi, lo = _split_bf16(e * inv)
        wide = (jnp.dot(hi, e_ref[...], preferred_element_type=F32)
                + jnp.dot(lo, e_ref[...], preferred_element_type=F32))
        merged = merged + wide * o_ref[...].astype(F32)
    x3 = x_ref[...] + jnp.dot(merged.astype(BF16), wo_ref[...], preferred_element_type=F32)
    x3_ref[...] = x3

    hi, lo = _split_bf16(_rms(x3, n_ref[...]))
    hm_ref[...] = hi
    logits = (jnp.dot(hi, rh_ref[...], preferred_element_type=F32)
              + jnp.dot(lo, rh_ref[...], preferred_element_type=F32)
              + jnp.dot(hi, rl_ref[...], preferred_element_type=F32))
    lane = lax.broadcasted_iota(jnp.int32, logits.shape, 1).astype(F32)
    lg = jnp.where(lane < N_EXPERTS, logits, -jnp.inf)
    m1 = jnp.max(lg, axis=-1, keepdims=True)
    i1 = jnp.min(jnp.where(lg == m1, lane, float(LANES)), axis=-1, keepdims=True)
    lg2 = jnp.where(lane == i1, -jnp.inf, lg)
    m2 = jnp.max(lg2, axis=-1, keepdims=True)
    i2 = jnp.min(jnp.where(lg2 == m2, lane, float(LANES)), axis=-1, keepdims=True)
    e2 = jnp.exp(m2 - m1)
    g1 = 1.0 / (1.0 + e2)
    g2 = e2 * g1
    rt_ref[...] = jnp.where(lane == 0, i1, jnp.where(lane == 1, i2, jnp.where(lane == 2, g1,
                            jnp.where(lane == 3, g2, 0.0))))


def merge_project_route(outs, lses, x, w_o, moe_gain, w_router):
    t = x.shape[0]
    tm = ROW_TILE // 2
    expand = (jnp.arange(LANES)[:, None] == (jnp.arange(D_MODEL)[None, :] // HEAD_DIM)).astype(BF16)
    wr = jnp.zeros((D_MODEL, LANES), F32).at[:, :N_EXPERTS].set(w_router.astype(F32))
    rh = wr.astype(BF16)
    rl = (wr - rh.astype(F32)).astype(BF16)
    row = lambda w: pl.BlockSpec((tm, w), lambda i: (i, 0))
    full = lambda a, b: pl.BlockSpec((a, b), lambda i: (0, 0))
    return pl.pallas_call(
        _merge_kernel,
        grid=(t // tm,),
        in_specs=[row(D_MODEL), row(D_MODEL), row(D_MODEL), row(LANES), row(LANES), row(LANES),
                  row(D_MODEL), full(D_MODEL, D_MODEL), full(LANES, D_MODEL), full(1, D_MODEL),
                  full(D_MODEL, LANES), full(D_MODEL, LANES)],
        out_specs=[row(D_MODEL), row(D_MODEL), row(LANES)],
        out_shape=[jax.ShapeDtypeStruct((t, D_MODEL), F32),
                   jax.ShapeDtypeStruct((t, D_MODEL), BF16),
                   jax.ShapeDtypeStruct((t, LANES), F32)],
        compiler_params=_params(("parallel",)),
        name="merge_project_route",
    )(*outs, *lses, x, w_o, expand, moe_gain.reshape(1, D_MODEL), rh, rl)


def _dispatch(top_idx, t):
    n_assign = 2 * t
    e_flat = top_idx.reshape(n_assign)
    onehot = (e_flat[:, None] == jnp.arange(N_EXPERTS, dtype=jnp.int32)[None, :]).astype(jnp.int32)
    csum = jnp.cumsum(onehot, axis=0)
    rank = jnp.take_along_axis(csum, e_flat[:, None], axis=1)[:, 0] - 1
    counts = csum[-1]
    padded = ((counts + MOE_ROWS - 1) // MOE_ROWS) * MOE_ROWS
    pend = jnp.cumsum(padded)
    pstart = pend - padded
    dest = pstart[e_flat] + rank
    n_blocks = n_assign // MOE_ROWS + N_EXPERTS
    n_rows = n_blocks * MOE_ROWS
    tok = jnp.arange(n_assign, dtype=jnp.int32) // 2
    row_tok = jnp.full((n_rows,), t, dtype=jnp.int32).at[dest].set(tok)
    block_start = jnp.arange(n_blocks, dtype=jnp.int32) * MOE_ROWS
    block_exp = jnp.clip(jnp.searchsorted(pend, block_start, side='right'), 0, N_EXPERTS - 1)
    return row_tok, dest.reshape(t, 2), block_exp.astype(jnp.int32)


def _trunk(x, w):
    bsz, seq, _ = x.shape
    t = bsz * seq
    nc = t // CHUNK

    xt = jnp.transpose(x.reshape(nc, CHUNK, D_MODEL), (1, 0, 2)).reshape(t, D_MODEL)
    ht = rmsnorm_rows(xt, w['ssm_norm']).reshape(CHUNK, nc, D_MODEL)
    cg = s5_states(ht, w['s5_in'], w['s5_lam'], seq // CHUNK)
    gt = s5_outputs(ht, cg, w['s5_toep'], w['s5_out'], w['ssm_d'])
    x1 = glu_residual(gt.reshape(t, D_MODEL), xt, w['w_glu'])
    x2 = ffn_residual(x1, w['ffn_norm'], w['ffn_gu'], w['ffn_down'])
    x2 = jnp.transpose(x2.reshape(CHUNK, nc, D_MODEL), (1, 0, 2)).reshape(t, D_MODEL)

    qkv = qkv_project(x2, w['att_norm'], w['w_qkv'], w['cos'][:seq], w['sin'][:seq], w['head_gains'], seq)
    outs, lses = [], []
    for g in range(len(DILATIONS)):
        o, l = dilated_attention(qkv, g, bsz, seq, w['band_bias'])
        outs.append(o)
        lses.append(l)
    x3, hm, route = merge_project_route(outs, lses, x2, w['w_o'], w['moe_norm'], w['w_router'])

    top_idx = route[:, 0:2].astype(jnp.int32)
    gates = route[:, 2:4]
    row_tok, dest, block_exp = _dispatch(top_idx, t)
    hm_ext = jnp.concatenate([hm, jnp.zeros((1, D_MODEL), hm.dtype)], axis=0)
    ys = expert_swiglu(hm_ext[row_tok], block_exp, w['moe_gu'], w['moe_down'])
    y = x3 + gates[:, 0:1] * ys[dest[:, 0]] + gates[:, 1:2] * ys[dest[:, 1]]
    return y.reshape(bsz, seq, D_MODEL)


def kernel(x_prompt, x_sample, ssm_norm, ssm_lam_re, ssm_lam_im, ssm_log_dt, ssm_b_re, ssm_b_im,
           ssm_c_re, ssm_c_im, ssm_d, ssm_w_glu, ffn_norm, ffn_w_gu, ffn_w_down, att_norm, att_w_qkv,
           att_q_gain, att_k_gain, att_w_o, moe_norm, moe_w_router, moe_w_gu, moe_w_down):
    s5_in, s5_lam, s5_toep, s5_out = _s5_tables(
        ssm_lam_re[0], ssm_lam_im[0], ssm_log_dt[0], ssm_b_re[0], ssm_b_im[0], ssm_c_re[0], ssm_c_im[0])

    perm = _quad_perm()
    wq = att_w_qkv[0].reshape(D_MODEL, 9, D_MODEL)
    wq = jnp.concatenate([wq[:, :6][:, :, perm], wq[:, 6:]], axis=1).reshape(D_MODEL, 9 * D_MODEL)

    max_seq = max(x_prompt.shape[1], x_sample.shape[1])
    inv_freq = ROPE_THETA ** (-jnp.arange(0, HEAD_DIM, 2, dtype=F32) / HEAD_DIM)
    ang = jnp.arange(max_seq, dtype=F32)[:, None] * inv_freq[None, :]
    cos_t = jnp.tile(jnp.cos(ang), (1, 4))
    sin_t = jnp.tile(jnp.sin(ang), (1, 4))
    half = HEAD_DIM // 2
    qg = att_q_gain[0].astype(F32) * (HEAD_DIM ** -0.5)
    kg = att_k_gain[0].astype(F32)
    head_gains = jnp.stack([
        jnp.stack([jnp.tile(qg[:half], 4), jnp.tile(qg[half:], 4)], 0),
        jnp.stack([jnp.tile(kg[:half], 4), jnp.tile(kg[half:], 4)], 0)], 0)

    w = dict(
        ssm_norm=ssm_norm[0], s5_in=s5_in, s5_lam=s5_lam, s5_toep=s5_toep, s5_out=s5_out, ssm_d=ssm_d[0],
        w_glu=ssm_w_glu[0].astype(BF16), ffn_norm=ffn_norm[0], ffn_gu=ffn_w_gu[0].astype(BF16),
        ffn_down=ffn_w_down[0].astype(BF16), att_norm=att_norm[0], w_qkv=wq.astype(BF16),
        cos=cos_t, sin=sin_t, head_gains=head_gains, band_bias=_band_bias(),
        w_o=att_w_o[0].astype(BF16), moe_norm=moe_norm[0], w_router=moe_w_router[0],
        moe_gu=moe_w_gu[0].astype(BF16), moe_down=moe_w_down[0].astype(BF16))
    return (_trunk(x_prompt, w), _trunk(x_sample, w))
```

```python
import functools
import math

import jax
import jax.numpy as jnp
from jax import lax
from jax.experimental import pallas as pl
from jax.experimental.pallas import tpu as pltpu

F32 = jnp.float32
BF16 = jnp.bfloat16

D_MODEL = 1024
SSM_GROUP = 16
SSM_GROUPS = 64
SSM_STATE = 64
HEAD_DIM = 64
N_HEADS = 16
DILATIONS = (1, 4, 16)
BAND_HALF = 64
ROPE_THETA = 10000.0
D_FF = 3584
N_EXPERTS = 8
RMS_EPS = 1e-6
NEG_INF = -1e30

CHUNK = 16
LANES = 128
GROUPS_PER_TILE = LANES // SSM_GROUP
N_GTILES = D_MODEL // LANES
S5_ROWS = 512
FF_TILE = 512
ROW_TILE = 1024
MOE_ROWS = 512
ATT_Q = 128
VMEM_LIMIT = 56 * 1024 * 1024


def _params(sem, vmem=VMEM_LIMIT):
    return pltpu.CompilerParams(dimension_semantics=sem, vmem_limit_bytes=vmem)


def _rms(x, gain):
    ms = jnp.mean(x * x, axis=-1, keepdims=True)
    return x * lax.rsqrt(ms + RMS_EPS) * gain


def _rmsnorm_kernel(x_ref, g_ref, o_ref):
    o_ref[...] = _rms(x_ref[...], g_ref[...]).astype(o_ref.dtype)


def rmsnorm_rows(x, gain):
    t = x.shape[0]
    return pl.pallas_call(
        _rmsnorm_kernel,
        grid=(t // ROW_TILE,),
        in_specs=[pl.BlockSpec((ROW_TILE, D_MODEL), lambda i: (i, 0)),
                  pl.BlockSpec((1, D_MODEL), lambda i: (0, 0))],
        out_specs=pl.BlockSpec((ROW_TILE, D_MODEL), lambda i: (i, 0)),
        out_shape=jax.ShapeDtypeStruct((t, D_MODEL), BF16),
        compiler_params=_params(("parallel",)),
        name="rmsnorm_rows",
    )(x, gain.reshape(1, D_MODEL))


def _s5_tables(lam_re, lam_im, log_dt, b_re, b_im, c_re, c_im):
    lam = jnp.minimum(lam_re.astype(F32), -1e-4) + 1j * lam_im.astype(F32)
    dt = jnp.exp(log_dt.astype(F32))[..., None]
    z = lam * dt
    lam_bar = jnp.exp(z)
    b_bar = ((lam_bar - 1.0) / lam)[..., None] * (b_re.astype(F32) + 1j * b_im.astype(F32))
    c_c = c_re.astype(F32) + 1j * c_im.astype(F32)
    steps = jnp.arange(CHUNK, dtype=F32)[:, None, None]

    def power(direction, exponent):
        return jnp.exp(z[direction][None] * exponent)

    pw = jnp.stack([power(0, steps), power(1, steps)], 1)
    a8 = (N_GTILES, GROUPS_PER_TILE)

    def widen(compact, onehot, row_group, col_group):
        wide = jnp.dot(compact, onehot, preferred_element_type=F32)
        return jnp.where(row_group[:, None] == col_group[None, :], wide, 0.0).astype(BF16)

    n_io = CHUNK * SSM_GROUP
    io = jnp.arange(n_io)
    col_tok = jnp.arange(2048)
    tok_onehot = ((io[:, None] // SSM_GROUP == col_tok[None, :] // LANES)
                  & (io[:, None] % SSM_GROUP == col_tok[None, :] % SSM_GROUP)).astype(F32)
    tok_group = (col_tok // SSM_GROUP) % GROUPS_PER_TILE
    n_st = 2 * SSM_STATE
    st = jnp.arange(n_st)
    col_st = jnp.arange(1024)
    st_onehot = ((st[:, None] // SSM_STATE == col_st[None, :] // 512)
                 & (st[:, None] % SSM_STATE == col_st[None, :] % SSM_STATE)).astype(F32)
    st_group = (col_st // SSM_STATE) % GROUPS_PER_TILE
    carry_group = (jnp.arange(2048) // SSM_STATE) % GROUPS_PER_TILE

    kd = jnp.einsum('dgop,kdgp,dgph->dkgoh', c_c, pw, b_bar).real
    ii = jnp.arange(CHUNK)[:, None]
    jj = jnp.arange(CHUNK)[None, :]
    kf = kd[0][jnp.clip(ii - jj, 0, CHUNK - 1)]
    kb = kd[1][jnp.clip(jj - ii, 0, CHUNK - 1)]
    sel = lambda m: m[:, :, None, None, None]
    kmat = jnp.where(sel(ii >= jj), kf, 0.0) + jnp.where(sel(jj >= ii), kb, 0.0)
    kmat = kmat.reshape(CHUNK, CHUNK, *a8, SSM_GROUP, SSM_GROUP)
    kc = jnp.transpose(kmat, (2, 1, 3, 5, 0, 4)).reshape(N_GTILES, 2048, n_io)
    w_toep = widen(kc, tok_onehot, tok_group, tok_group)

    cf = power(0, CHUNK - 1.0 - steps)
    cb = power(1, steps)
    coef = jnp.stack([cf[..., None] * b_bar[0][None], cb[..., None] * b_bar[1][None]], 0)
    coef = jnp.stack([coef.real, coef.imag], 0).reshape(2, 2, CHUNK, *a8, SSM_STATE, SSM_GROUP)
    xc = jnp.transpose(coef, (3, 1, 2, 4, 6, 0, 5)).reshape(N_GTILES, 2, 2048, n_st)
    w_in = widen(xc, st_onehot, tok_group, st_group)

    pf = power(0, steps + 1.0)
    pb = power(1, CHUNK - steps)
    w = jnp.stack([c_c[0][None] * pf[:, :, None, :], c_c[1][None] * pb[:, :, None, :]], 0)
    w = jnp.stack([w.real, -w.imag], 0).reshape(2, 2, CHUNK, *a8, SSM_GROUP, SSM_STATE)
    wc = jnp.transpose(w, (3, 1, 0, 4, 6, 2, 5)).reshape(N_GTILES, 2048, n_io)
    w_out = widen(wc, tok_onehot, carry_group, tok_group)

    lc = jnp.exp(z * float(CHUNK))
    lam_c = jnp.stack([lc.real, lc.imag], 0).reshape(2, 2, *a8, SSM_STATE)
    lam_c = jnp.transpose(lam_c, (2, 1, 0, 3, 4)).reshape(N_GTILES, 2, 2, 512)
    return w_in.astype(BF16), lam_c, w_toep.astype(BF16), w_out.astype(BF16)


def _chunk_scan(xr, xi, ar, ai, seq_rows, reverse):
    n = xr.shape[0]
    row = lax.broadcasted_iota(jnp.int32, xr.shape, 0) % seq_rows

    def shift(x, sh):
        if reverse:
            y = pltpu.roll(x, n - sh, 0)
            return jnp.where(row < seq_rows - sh, y, 0.0)
        y = pltpu.roll(x, sh, 0)
        return jnp.where(row >= sh, y, 0.0)

    xr, xi = shift(xr, 1), shift(xi, 1)
    sh = 1
    while sh < seq_rows:
        yr, yi = shift(xr, sh), shift(xi, sh)
        xr, xi = xr + ar * yr - ai * yi, xi + ar * yi + ai * yr
        ar, ai = ar * ar - ai * ai, 2.0 * ar * ai
        sh *= 2
    return xr, xi


def _s5a_kernel(h_ref, win_ref, lam_ref, c_ref, *, seq_rows):
    u = jnp.concatenate([h_ref[t] for t in range(CHUNK)], axis=-1)
    s = jnp.dot(u, win_ref[...], preferred_element_type=F32)
    half = s.shape[1] // 2
    ar = lam_ref[0:1, :]
    ai = lam_ref[1:2, :]
    for direction in range(2):
        @pl.when(pl.program_id(1) == direction)
        def _(direction=direction):
            xr, xi = _chunk_scan(s[:, :half], s[:, half:], ar, ai, seq_rows, direction == 1)
            c_ref[:, :half] = xr.astype(c_ref.dtype)
            c_ref[:, half:] = xi.astype(c_ref.dtype)


def s5_states(ht, w_in, lam_c, seq_rows):
    nc = ht.shape[1]
    n = S5_ROWS
    return pl.pallas_call(
        functools.partial(_s5a_kernel, seq_rows=seq_rows),
        grid=(N_GTILES, 2, nc // n),
        in_specs=[pl.BlockSpec((CHUNK, n, LANES), lambda a, d, r: (0, r, a)),
                  pl.BlockSpec((None, None, 2048, 1024), lambda a, d, r: (a, d, 0, 0)),
                  pl.BlockSpec((None, None, 2, 512), lambda a, d, r: (a, d, 0, 0))],
        out_specs=pl.BlockSpec((None, None, n, 1024), lambda a, d, r: (a, d, r, 0)),
        out_shape=jax.ShapeDtypeStruct((N_GTILES, 2, nc, 1024), BF16),
        compiler_params=_params(("parallel", "parallel", "parallel")),
        name="s5_states",
    )(ht, w_in, lam_c)


def _gelu_tanh(x):
    return 0.5 * x * (1.0 + jnp.tanh(math.sqrt(2.0 / math.pi) * (x + 0.044715 * (x * x * x))))


def _s5b_kernel(h_ref, c_ref, wt_ref, wo_ref, d_ref, o_ref):
    half_steps = o_ref.shape[0]
    u = jnp.concatenate([h_ref[t] for t in range(CHUNK)], axis=-1)
    cc = jnp.concatenate([c_ref[0], c_ref[1]], axis=-1)
    y = (jnp.dot(u, wt_ref[...], preferred_element_type=F32)
         + jnp.dot(cc, wo_ref[...], preferred_element_type=F32))
    base = pl.program_id(1) * half_steps
    for i in range(half_steps):
        ht = h_ref[base + i].astype(F32)
        yi = y[:, i * LANES:(i + 1) * LANES] + d_ref[...] * ht
        o_ref[i] = _gelu_tanh(yi).astype(o_ref.dtype)


def s5_outputs(ht, cg, w_toep, w_out, d_skip):
    nc = ht.shape[1]
    n = S5_ROWS
    hs = CHUNK // 2
    return pl.pallas_call(
        _s5b_kernel,
        grid=(N_GTILES, 2, nc // n),
        in_specs=[pl.BlockSpec((CHUNK, n, LANES), lambda a, h, r: (0, r, a)),
                  pl.BlockSpec((None, 2, n, 1024), lambda a, h, r: (a, 0, r, 0)),
                  pl.BlockSpec((None, 2048, 1024), lambda a, h, r: (a, 0, h)),
                  pl.BlockSpec((None, 2048, 1024), lambda a, h, r: (a, 0, h)),
                  pl.BlockSpec((None, 1, LANES), lambda a, h, r: (a, 0, 0))],
        out_specs=pl.BlockSpec((hs, n, LANES), lambda a, h, r: (h, r, a)),
        out_shape=jax.ShapeDtypeStruct((CHUNK, nc, D_MODEL), BF16),
        compiler_params=_params(("parallel", "parallel", "parallel")),
        name="s5_outputs",
    )(ht, cg, w_toep, w_out, d_skip.astype(F32).reshape(N_GTILES, 1, LANES))


def _sigmoid(x):
    return 1.0 / (1.0 + jnp.exp(-x))


def _glu_kernel(g_ref, x_ref, w_ref, o_ref):
    z = jnp.dot(g_ref[...], w_ref[...], preferred_element_type=F32)
    o_ref[...] = x_ref[...] + z[:, :D_MODEL] * _sigmoid(z[:, D_MODEL:])


def glu_residual(g, x, w_glu):
    t = x.shape[0]
    tm = ROW_TILE // 2
    return pl.pallas_call(
        _glu_kernel,
        grid=(t // tm,),
        in_specs=[pl.BlockSpec((tm, D_MODEL), lambda i: (i, 0)),
                  pl.BlockSpec((tm, D_MODEL), lambda i: (i, 0)),
                  pl.BlockSpec((D_MODEL, 2 * D_MODEL), lambda i: (0, 0))],
        out_specs=pl.BlockSpec((tm, D_MODEL), lambda i: (i, 0)),
        out_shape=jax.ShapeDtypeStruct((t, D_MODEL), F32),
        compiler_params=_params(("parallel",)),
        name="glu_residual",
    )(g, x, w_glu)


def _swiglu_step(h, wg_ref, wu_ref, wd_ref, acc_ref):
    g = jnp.dot(h, wg_ref[...], preferred_element_type=F32)
    u = jnp.dot(h, wu_ref[...], preferred_element_type=F32)
    a = (g * _sigmoid(g) * u).astype(BF16)
    acc_ref[...] += jnp.dot(a, wd_ref[...], preferred_element_type=F32)


def _ffn_kernel(x_ref, n_ref, wg_ref, wu_ref, wd_ref, o_ref, h_scr, acc_scr):
    f = pl.program_id(1)

    @pl.when(f == 0)
    def _():
        h_scr[...] = _rms(x_ref[...], n_ref[...]).astype(BF16)
        acc_scr[...] = jnp.zeros_like(acc_scr)

    _swiglu_step(h_scr[...], wg_ref, wu_ref, wd_ref, acc_scr)

    @pl.when(f == pl.num_programs(1) - 1)
    def _():
        o_ref[...] = x_ref[...] + acc_scr[...]


def ffn_residual(x, gain, w_gu, w_down):
    t = x.shape[0]
    nf = D_FF // FF_TILE
    return pl.pallas_call(
        _ffn_kernel,
        grid=(t // ROW_TILE, nf),
        in_specs=[pl.BlockSpec((ROW_TILE, D_MODEL), lambda i, f: (i, 0)),
                  pl.BlockSpec((1, D_MODEL), lambda i, f: (0, 0)),
                  pl.BlockSpec((D_MODEL, FF_TILE), lambda i, f: (0, f)),
                  pl.BlockSpec((D_MODEL, FF_TILE), lambda i, f: (0, nf + f)),
                  pl.BlockSpec((FF_TILE, D_MODEL), lambda i, f: (f, 0))],
        out_specs=pl.BlockSpec((ROW_TILE, D_MODEL), lambda i, f: (i, 0)),
        out_shape=jax.ShapeDtypeStruct((t, D_MODEL), F32),
        scratch_shapes=[pltpu.VMEM((ROW_TILE, D_MODEL), BF16),
                        pltpu.VMEM((ROW_TILE, D_MODEL), F32)],
        compiler_params=_params(("parallel", "arbitrary")),
        name="ffn_residual",
    )(x, gain.reshape(1, D_MODEL), w_gu, w_gu, w_down)


def _expert_kernel(be_ref, x_ref, wg_ref, wu_ref, wd_ref, o_ref, acc_scr):
    f = pl.program_id(1)

    @pl.when(f == 0)
    def _():
        acc_scr[...] = jnp.zeros_like(acc_scr)

    _swiglu_step(x_ref[...], wg_ref, wu_ref, wd_ref, acc_scr)

    @pl.when(f == pl.num_programs(1) - 1)
    def _():
        o_ref[...] = acc_scr[...]


def expert_swiglu(xs, block_exp, w_gu, w_down):
    rows = xs.shape[0]
    nf = D_FF // FF_TILE
    grid_spec = pltpu.PrefetchScalarGridSpec(
        num_scalar_prefetch=1,
        grid=(rows // MOE_ROWS, nf),
        in_specs=[pl.BlockSpec((MOE_ROWS, D_MODEL), lambda i, f, be: (i, 0)),
                  pl.BlockSpec((None, D_MODEL, FF_TILE), lambda i, f, be: (be[i], 0, f)),
                  pl.BlockSpec((None, D_MODEL, FF_TILE), lambda i, f, be: (be[i], 0, nf + f)),
                  pl.BlockSpec((None, FF_TILE, D_MODEL), lambda i, f, be: (be[i], f, 0))],
        out_specs=pl.BlockSpec((MOE_ROWS, D_MODEL), lambda i, f, be: (i, 0)),
        scratch_shapes=[pltpu.VMEM((MOE_ROWS, D_MODEL), F32)],
    )
    return pl.pallas_call(
        _expert_kernel,
        grid_spec=grid_spec,
        out_shape=jax.ShapeDtypeStruct((rows, D_MODEL), F32),
        compiler_params=_params(("parallel", "arbitrary")),
        name="expert_swiglu",
    )(block_exp, xs, w_gu, w_gu, w_down)


def _quad_perm():
    idx = []
    for c in range(N_HEADS // 4):
        for half in range(2):
            for j in range(4):
                for f in range(HEAD_DIM // 2):
                    idx.append((4 * c + j) * HEAD_DIM + half * (HEAD_DIM // 2) + f)
    return jnp.asarray(idx, dtype=jnp.int32)


def _qkv_kernel(x_ref, n_ref, w_ref, cos_ref, sin_ref, hg_ref, ones_ref, o0_ref, o1_ref, o2_ref,
                h_scr, slab_scr):
    j = pl.program_id(1)
    group = j // 3
    which = j - 3 * group
    n_slabs = D_MODEL // LANES

    @pl.when(j == 0)
    def _():
        h_scr[...] = _rms(x_ref[...], n_ref[...]).astype(BF16)

    acc = jnp.dot(h_scr[...], w_ref[...], preferred_element_type=F32)

    @pl.when(which == 2)
    def _():
        for s in range(n_slabs):
            slab_scr[s] = acc[:, s * LANES:(s + 1) * LANES]

    @pl.when(which < 2)
    def _():
        gains = hg_ref[which]
        ga, gb = gains[0:1, :], gains[1:2, :]
        cos, sin = cos_ref[...], sin_ref[...]
        for c in range(N_HEADS // 4):
            lo = c * 2 * LANES
            a = acc[:, lo:lo + LANES]
            b = acc[:, lo + LANES:lo + 2 * LANES]
            sq = jnp.concatenate([a * a, b * b], axis=-1).astype(BF16)
            ss = jnp.dot(sq, ones_ref[...], preferred_element_type=F32)
            inv = lax.rsqrt(ss * (1.0 / HEAD_DIM) + RMS_EPS)
            an = a * inv * ga
            bn = b * inv * gb
            slab_scr[2 * c] = an * cos - bn * sin
            slab_scr[2 * c + 1] = bn * cos + an * sin

    for g, o_ref in enumerate((o0_ref, o1_ref, o2_ref)):
        @pl.when(group == g)
        def _(g=g, o_ref=o_ref):
            dil = DILATIONS[g]
            rows = o_ref.shape[1]
            for r in range(dil):
                for s in range(n_slabs):
                    if dil == 1:
                        piece = slab_scr[s]
                    else:
                        piece = slab_scr[s, pl.ds(r, rows, stride=dil), :]
                    o_ref[r, :, s * LANES:(s + 1) * LANES] = piece.astype(o_ref.dtype)


def qkv_project(x, gain, w_qkv_p, cos_t, sin_t, head_gains, bsz, seq):
    t = x.shape[0]
    tm = ROW_TILE
    lane = jnp.arange(2 * LANES)
    ones = ((lane[:, None] % LANES) // 32 == (jnp.arange(LANES)[None, :] // 32)).astype(BF16)
    pos_blocks = seq // tm

    def out_spec(g):
        dil = DILATIONS[g]
        return pl.BlockSpec((None, dil, tm // dil, D_MODEL),
                            lambda i, j: (i // pos_blocks, 0, i % pos_blocks, jnp.clip(j - 3 * g, 0, 2)))

    return pl.pallas_call(
        _qkv_kernel,
        grid=(t // tm, 9),
        in_specs=[pl.BlockSpec((tm, D_MODEL), lambda i, j: (i, 0)),
                  pl.BlockSpec((1, D_MODEL), lambda i, j: (0, 0)),
                  pl.BlockSpec((D_MODEL, D_MODEL), lambda i, j: (0, j)),
                  pl.BlockSpec((tm, LANES), lambda i, j: (i % pos_blocks, 0)),
                  pl.BlockSpec((tm, LANES), lambda i, j: (i % pos_blocks, 0)),
                  pl.BlockSpec((2, 2, LANES), lambda i, j: (0, 0, 0)),
                  pl.BlockSpec((2 * LANES, LANES), lambda i, j: (0, 0))],
        out_specs=[out_spec(g) for g in range(len(DILATIONS))],
        out_shape=[jax.ShapeDtypeStruct((bsz, d, seq // d, 3 * D_MODEL), BF16) for d in DILATIONS],
        scratch_shapes=[pltpu.VMEM((tm, D_MODEL), BF16),
                        pltpu.VMEM((D_MODEL // LANES, tm, LANES), F32)],
        compiler_params=_params(("parallel", "arbitrary")),
        name="qkv_project",
    )(x, gain.reshape(1, D_MODEL), w_qkv_p, cos_t, sin_t, head_gains, ones)


def _attn_kernel(q_ref, kp_ref, kc_ref, kn_ref, vp_ref, vc_ref, vn_ref, bias_ref, o_ref, lse_ref):
    quad = 2 * LANES
    lane_q = lax.broadcasted_iota(jnp.int32, (1, quad), 1)
    q_head = (lane_q % LANES) // 32
    v_head = lane_q // HEAD_DIM
    lane_l = lax.broadcasted_iota(jnp.int32, (1, LANES), 1)
    bias = bias_ref[...]
    lse_acc = jnp.zeros(lse_ref.shape, F32)
    for c in range(N_HEADS // 4):
        sl = slice(c * quad, (c + 1) * quad)
        q = q_ref[:, sl]
        k = jnp.concatenate([kp_ref[:, sl], kc_ref[:, sl], kn_ref[:, sl]], axis=0)
        v = jnp.concatenate([vp_ref[:, sl], vc_ref[:, sl], vn_ref[:, sl]], axis=0)
        probs, vparts = [], []
        scale = jnp.zeros((q.shape[0], quad), F32)
        for j in range(4):
            qm = jnp.where(q_head == j, q, jnp.zeros_like(q))
            s = lax.dot_general(qm, k, (((1,), (1,)), ((), ())), preferred_element_type=F32) + bias
            m = jnp.max(s, axis=-1, keepdims=True)
            p = jnp.exp(s - m)
            l = jnp.sum(p, axis=-1, keepdims=True)
            probs.append(p.astype(BF16))
            vparts.append(jnp.where(v_head == j, v, jnp.zeros_like(v)))
            scale = jnp.where(v_head == j, 1.0 / l, scale)
            lse_acc = jnp.where(lane_l == 4 * c + j, m + jnp.log(l), lse_acc)
        pv = jnp.dot(jnp.concatenate(probs, axis=1), jnp.concatenate(vparts, axis=0),
                     preferred_element_type=F32)
        o_ref[:, sl] = (pv * scale).astype(o_ref.dtype)
    lse_ref[...] = lse_acc


def _band_bias():
    qa = jnp.arange(ATT_Q)[:, None]
    kc = jnp.arange(ATT_Q + 2 * BAND_HALF)[None, :]
    band = jnp.abs(kc - BAND_HALF - qa) <= BAND_HALF
    first = band & (kc >= BAND_HALF)
    last = band & (kc < BAND_HALF + ATT_Q)
    bias = jnp.stack([first, band, last], 0)
    return jnp.where(bias, 0.0, NEG_INF).astype(F32)


def dilated_attention(qkv, bias):
    bsz, dil, n_sub, _ = qkv.shape
    n_i = n_sub // ATT_Q
    half_blocks = ATT_Q // BAND_HALF

    def cur(which):
        return pl.BlockSpec((None, None, ATT_Q, D_MODEL), lambda b, r, i: (b, r, i, which))

    def prev(which):
        return pl.BlockSpec((None, None, BAND_HALF, D_MODEL),
                            lambda b, r, i: (b, r, jnp.maximum(i * half_blocks - 1, 0), which))

    def nxt(which):
        return pl.BlockSpec((None, None, BAND_HALF, D_MODEL),
                            lambda b, r, i: (b, r, jnp.minimum((i + 1) * half_blocks, n_sub // BAND_HALF - 1), which))

    bias_spec = pl.BlockSpec(
        (None, ATT_Q, ATT_Q + 2 * BAND_HALF),
        lambda b, r, i: (jnp.where(i == 0, 0, jnp.where(i == n_i - 1, 2, 1)), 0, 0))
    return pl.pallas_call(
        _attn_kernel,
        grid=(bsz, dil, n_i),
        in_specs=[cur(0), prev(1), cur(1), nxt(1), prev(2), cur(2), nxt(2), bias_spec],
        out_specs=[pl.BlockSpec((None, None, ATT_Q, D_MODEL), lambda b, r, i: (b, r, i, 0)),
                   pl.BlockSpec((None, None, ATT_Q, LANES), lambda b, r, i: (b, r, i, 0))],
        out_shape=[jax.ShapeDtypeStruct((bsz, dil, n_sub, D_MODEL), BF16),
                   jax.ShapeDtypeStruct((bsz, dil, n_sub, LANES), F32)],
        compiler_params=_params(("parallel", "parallel", "parallel")),
        name=f"dilated_attention_{dil}",
    )(qkv, qkv, qkv, qkv, qkv, qkv, qkv, bias)


def _split_bf16(x):
    hi = x.astype(BF16)
    return hi, (x - hi.astype(F32)).astype(BF16)


def _merge_kernel(o0_ref, o1_ref, o2_ref, l0_ref, l1_ref, l2_ref, x_ref, wo_ref, e_ref, n_ref,
                  rh_ref, rl_ref, x3_ref, hm_ref, rt_ref, oslab_scr, lse_scr):
    n_slabs = D_MODEL // LANES
    outs, lses = [o0_ref[0].astype(F32)], [l0_ref[0]]
    for g, (o_ref, l_ref) in enumerate(((o1_ref, l1_ref), (o2_ref, l2_ref))):
        dil, rows = o_ref.shape[0], o_ref.shape[1]
        for r in range(dil):
            lse_scr[g, pl.ds(r, rows, stride=dil), :] = l_ref[r]
            for s in range(n_slabs):
                oslab_scr[g, s, pl.ds(r, rows, stride=dil), :] = o_ref[r, :, s * LANES:(s + 1) * LANES].astype(F32)
        outs.append(jnp.concatenate([oslab_scr[g, s] for s in range(n_slabs)], axis=-1))
        lses.append(lse_scr[g])

    la, lb, lc = lses
    m = jnp.maximum(jnp.maximum(la, lb), lc)
    ea, eb, ec = jnp.exp(la - m), jnp.exp(lb - m), jnp.exp(lc - m)
    inv = 1.0 / (ea + eb + ec)
    merged = jnp.zeros(x_ref.shape, F32)
    for e, o in zip((ea, eb, ec), outs):
        hi, lo = _split_bf16(e * inv)
        wide = (jnp.dot(hi, e_ref[...], preferred_element_type=F32)
                + jnp.dot(lo, e_ref[...], preferred_element_type=F32))
        merged = merged + wide * o
    x3 = x_ref[...] + jnp.dot(merged.astype(BF16), wo_ref[...], preferred_element_type=F32)
    x3_ref[...] = x3

    hi, lo = _split_bf16(_rms(x3, n_ref[...]))
    hm_ref[...] = hi
    logits = (jnp.dot(hi, rh_ref[...], preferred_element_type=F32)
              + jnp.dot(lo, rh_ref[...], preferred_element_type=F32)
              + jnp.dot(hi, rl_ref[...], preferred_element_type=F32))
    lane = lax.broadcasted_iota(jnp.int32, logits.shape, 1).astype(F32)
    lg = jnp.where(lane < N_EXPERTS, logits, -jnp.inf)
    m1 = jnp.max(lg, axis=-1, keepdims=True)
    i1 = jnp.min(jnp.where(lg == m1, lane, float(LANES)), axis=-1, keepdims=True)
    lg2 = jnp.where(lane == i1, -jnp.inf, lg)
    m2 = jnp.max(lg2, axis=-1, keepdims=True)
    i2 = jnp.min(jnp.where(lg2 == m2, lane, float(LANES)), axis=-1, keepdims=True)
    e2 = jnp.exp(m2 - m1)
    g1 = 1.0 / (1.0 + e2)
    g2 = e2 * g1
    rt_ref[...] = jnp.where(lane == 0, i1, jnp.where(lane == 1, i2, jnp.where(lane == 2, g1,
                            jnp.where(lane == 3, g2, 0.0))))


def merge_project_route(outs, lses, x, w_o, moe_gain, w_router, seq):
    t = x.shape[0]
    tm = ROW_TILE // 2
    seq_blocks = seq // tm
    expand = (jnp.arange(LANES)[:, None] == (jnp.arange(D_MODEL)[None, :] // HEAD_DIM)).astype(BF16)
    wr = jnp.zeros((D_MODEL, LANES), F32).at[:, :N_EXPERTS].set(w_router.astype(F32))
    rh = wr.astype(BF16)
    rl = (wr - rh.astype(F32)).astype(BF16)
    row = lambda w: pl.BlockSpec((tm, w), lambda i: (i, 0))
    full = lambda a, b: pl.BlockSpec((a, b), lambda i: (0, 0))
    grouped = lambda dil, w: pl.BlockSpec((None, dil, tm // dil, w),
                                          lambda i: (i // seq_blocks, 0, i % seq_blocks, 0))
    return pl.pallas_call(
        _merge_kernel,
        grid=(t // tm,),
        in_specs=[grouped(d, D_MODEL) for d in DILATIONS] + [grouped(d, LANES) for d in DILATIONS] + [
                  row(D_MODEL), full(D_MODEL, D_MODEL), full(LANES, D_MODEL), full(1, D_MODEL),
                  full(D_MODEL, LANES), full(D_MODEL, LANES)],
        out_specs=[row(D_MODEL), row(D_MODEL), row(LANES)],
        out_shape=[jax.ShapeDtypeStruct((t, D_MODEL), F32),
                   jax.ShapeDtypeStruct((t, D_MODEL), BF16),
                   jax.ShapeDtypeStruct((t, LANES), F32)],
        scratch_shapes=[pltpu.VMEM((2, D_MODEL // LANES, tm, LANES), F32),
                        pltpu.VMEM((2, tm, LANES), F32)],
        compiler_params=_params(("parallel",)),
        name="merge_project_route",
    )(*outs, *lses, x, w_o, expand, moe_gain.reshape(1, D_MODEL), rh, rl)


def _dispatch(top_idx, t):
    n_assign = 2 * t
    e_flat = top_idx.reshape(n_assign)
    onehot = (e_flat[:, None] == jnp.arange(N_EXPERTS, dtype=jnp.int32)[None, :]).astype(jnp.int32)
    csum = jnp.cumsum(onehot, axis=0)
    rank = jnp.take_along_axis(csum, e_flat[:, None], axis=1)[:, 0] - 1
    counts = csum[-1]
    padded = ((counts + MOE_ROWS - 1) // MOE_ROWS) * MOE_ROWS
    pend = jnp.cumsum(padded)
    pstart = pend - padded
    dest = pstart[e_flat] + rank
    n_blocks = n_assign // MOE_ROWS + N_EXPERTS
    n_rows = n_blocks * MOE_ROWS
    tok = jnp.arange(n_assign, dtype=jnp.int32) // 2
    row_tok = jnp.full((n_rows,), t, dtype=jnp.int32).at[dest].set(tok)
    block_start = jnp.arange(n_blocks, dtype=jnp.int32) * MOE_ROWS
    block_exp = jnp.clip(jnp.searchsorted(pend, block_start, side='right'), 0, N_EXPERTS - 1)
    return row_tok, dest.reshape(t, 2), block_exp.astype(jnp.int32)


def _trunk(x, w):
    bsz, seq, _ = x.shape
    t = bsz * seq
    nc = t // CHUNK

    xt = jnp.transpose(x.reshape(nc, CHUNK, D_MODEL), (1, 0, 2)).reshape(t, D_MODEL)
    ht = rmsnorm_rows(xt, w['ssm_norm']).reshape(CHUNK, nc, D_MODEL)
    cg = s5_states(ht, w['s5_in'], w['s5_lam'], seq // CHUNK)
    gt = s5_outputs(ht, cg, w['s5_toep'], w['s5_out'], w['ssm_d'])
    x1 = glu_residual(gt.reshape(t, D_MODEL), xt, w['w_glu'])
    x2 = ffn_residual(x1, w['ffn_norm'], w['ffn_gu'], w['ffn_down'])
    x2 = jnp.transpose(x2.reshape(CHUNK, nc, D_MODEL), (1, 0, 2)).reshape(t, D_MODEL)

    qkvs = qkv_project(x2, w['att_norm'], w['w_qkv'], w['cos'][:seq], w['sin'][:seq], w['head_gains'], bsz, seq)
    outs, lses = [], []
    for qkv in qkvs:
        o, l = dilated_attention(qkv, w['band_bias'])
        outs.append(o)
        lses.append(l)
    x3, hm, route = merge_project_route(outs, lses, x2, w['w_o'], w['moe_norm'], w['w_router'], seq)

    top_idx = route[:, 0:2].astype(jnp.int32)
    gates = route[:, 2:4]
    row_tok, dest, block_exp = _dispatch(top_idx, t)
    hm_ext = jnp.concatenate([hm, jnp.zeros((1, D_MODEL), hm.dtype)], axis=0)
    ys = expert_swiglu(hm_ext[row_tok], block_exp, w['moe_gu'], w['moe_down'])
    y = x3 + gates[:, 0:1] * ys[dest[:, 0]] + gates[:, 1:2] * ys[dest[:, 1]]
    return y.reshape(bsz, seq, D_MODEL)


def kernel(x_prompt, x_sample, ssm_norm, ssm_lam_re, ssm_lam_im, ssm_log_dt, ssm_b_re, ssm_b_im,
           ssm_c_re, ssm_c_im, ssm_d, ssm_w_glu, ffn_norm, ffn_w_gu, ffn_w_down, att_norm, att_w_qkv,
           att_q_gain, att_k_gain, att_w_o, moe_norm, moe_w_router, moe_w_gu, moe_w_down):
    s5_in, s5_lam, s5_toep, s5_out = _s5_tables(
        ssm_lam_re[0], ssm_lam_im[0], ssm_log_dt[0], ssm_b_re[0], ssm_b_im[0], ssm_c_re[0], ssm_c_im[0])

    perm = _quad_perm()
    wq = att_w_qkv[0].reshape(D_MODEL, 3, len(DILATIONS), D_MODEL)
    wq = jnp.concatenate([wq[:, :2][..., perm], wq[:, 2:]], axis=1)
    wq = jnp.transpose(wq, (0, 2, 1, 3)).reshape(D_MODEL, 9 * D_MODEL)

    max_seq = max(x_prompt.shape[1], x_sample.shape[1])
    inv_freq = ROPE_THETA ** (-jnp.arange(0, HEAD_DIM, 2, dtype=F32) / HEAD_DIM)
    ang = jnp.arange(max_seq, dtype=F32)[:, None] * inv_freq[None, :]
    cos_t = jnp.tile(jnp.cos(ang), (1, 4))
    sin_t = jnp.tile(jnp.sin(ang), (1, 4))
    half = HEAD_DIM // 2
    qg = att_q_gain[0].astype(F32) * (HEAD_DIM ** -0.5)
    kg = att_k_gain[0].astype(F32)
    head_gains = jnp.stack([
        jnp.stack([jnp.tile(qg[:half], 4), jnp.tile(qg[half:], 4)], 0),
        jnp.stack([jnp.tile(kg[:half], 4), jnp.tile(kg[half:], 4)], 0)], 0)

    w = dict(
        ssm_norm=ssm_norm[0], s5_in=s5_in, s5_lam=s5_lam, s5_toep=s5_toep, s5_out=s5_out, ssm_d=ssm_d[0],
        w_glu=ssm_w_glu[0].astype(BF16), ffn_norm=ffn_norm[0], ffn_gu=ffn_w_gu[0].astype(BF16),
        ffn_down=ffn_w_down[0].astype(BF16), att_norm=att_norm[0], w_qkv=wq.astype(BF16),
        cos=cos_t, sin=sin_t, head_gains=head_gains, band_bias=_band_bias(),
        w_o=att_w_o[0].astype(BF16), moe_norm=moe_norm[0], w_router=moe_w_router[0],
        moe_gu=moe_w_gu[0].astype(BF16), moe_down=moe_w_down[0].astype(BF16))
    return (_trunk(x_prompt, w), _trunk(x_sample, w))
```
